```python
import jax, jax.numpy as jnp
from jax import lax
import numpy as np

D_MODEL = 1024
BATCH = 8
SEQ = 2048
DEPTH = 1
DEC_BATCH = 128
DEC_SEQ = 4
PAST_LEN = 16384
PAGE_SIZE = 128

MIX_WIDTH = D_MODEL
POOL_WIDTH = MIX_WIDTH // 2
CONV_WIDTH = MIX_WIDTH - POOL_WIDTH
POOL_WINDOWS = (2, 4, 8, 16)
N_POOL_GROUPS = len(POOL_WINDOWS)
POOL_GROUP = POOL_WIDTH // N_POOL_GROUPS
POOL_STATE = max(POOL_WINDOWS) - 1
CONV_KERNEL = 31
CONV_STATE = CONV_KERNEL - 1
N_IN = POOL_WIDTH + 2 * CONV_WIDTH
PEER_HEADS = 8
PEER_KEY_DIM = 256
PEER_HALF = PEER_KEY_DIM // 2
N_KEYS = 128
N_EXPERTS = N_KEYS * N_KEYS
PEER_TOPK = 16
PEER_BLOCK = 128
N_MOD = 6
EPS = 1e-6

kernel_name = 'hybrid_pool_conformer_peer_adaln_step'


def rmsnorm(x, g):
    xf = x.astype(jnp.float32)
    y = xf * lax.rsqrt(jnp.mean(xf * xf, axis=-1, keepdims=True) + EPS)
    return (y * g.astype(jnp.float32)).astype(x.dtype)


def layernorm(x, g, b):
    xf = x.astype(jnp.float32)
    mu = jnp.mean(xf, axis=-1, keepdims=True)
    var = jnp.mean(jnp.square(xf - mu), axis=-1, keepdims=True)
    y = (xf - mu) * lax.rsqrt(var + EPS)
    return (y * g.astype(jnp.float32) + b.astype(jnp.float32)).astype(x.dtype)


def token_mixer(h, pool_prefix, conv_prefix, start, w_in, b_in, w_pool, pool_scale,
                conv_w, conv_b, ln_g, ln_b, w_out, b_out):
    B, T, _ = h.shape
    proj = h @ w_in + b_in
    hp = proj[..., :POOL_WIDTH]
    cv = proj[..., POOL_WIDTH:POOL_WIDTH + CONV_WIDTH]
    cg = proj[..., POOL_WIDTH + CONV_WIDTH:]

    zp = jnp.concatenate([pool_prefix.astype(hp.dtype), hp], axis=1)
    cs = jnp.pad(jnp.cumsum(zp.astype(jnp.float32), axis=1), ((0, 0), (1, 0), (0, 0)))
    pos = start + jnp.arange(T, dtype=jnp.int32)
    P = POOL_STATE
    diffs = []
    for gi, w in enumerate(POOL_WINDOWS):
        sl = slice(gi * POOL_GROUP, (gi + 1) * POOL_GROUP)
        wsum = cs[:, P + 1:P + T + 1, sl] - cs[:, P + 1 - w:P + T + 1 - w, sl]
        cnt = jnp.minimum(pos + 1, w).astype(jnp.float32)[None, :, None]
        diffs.append(wsum / cnt - hp[..., sl].astype(jnp.float32))
    d = jnp.stack(diffs, axis=2).astype(hp.dtype)
    pool_out = jnp.einsum('btgc,gcd->btgd', d, w_pool).reshape(B, T, POOL_WIDTH) * pool_scale

    glu = cv * jax.nn.sigmoid(cg)
    zc = jnp.concatenate([conv_prefix.astype(glu.dtype), glu], axis=1)
    conv = lax.conv_general_dilated(zc, conv_w.astype(zc.dtype), (1,), 'VALID',
                                    dimension_numbers=('NWC', 'WIO', 'NWC'),
                                    feature_group_count=CONV_WIDTH) + conv_b
    conv_out = jax.nn.silu(layernorm(conv, ln_g, ln_b))

    mixed = jnp.concatenate([pool_out, conv_out], axis=-1) @ w_out + b_out
    return mixed, zp[:, -POOL_STATE:], zc[:, -CONV_STATE:]


def peer_ffn(h, w_query, sub_keys, u_table, v_table):
    B, T, D = h.shape
    n = B * T
    pad = (-n) % PEER_BLOCK
    hf = jnp.pad(h.reshape(n, D), ((0, pad), (0, 0)))
    q = (hf @ w_query).reshape(-1, PEER_HEADS, 2, PEER_HALF)
    s = jnp.einsum('nhpk,hpmk->nhpm', q.astype(jnp.float32), sub_keys.astype(jnp.float32))
    sv, si = lax.top_k(s, PEER_TOPK)
    comb = (sv[:, :, 0, :, None] + sv[:, :, 1, None, :]).reshape(-1, PEER_HEADS, PEER_TOPK * PEER_TOPK)
    cidx = (si[:, :, 0, :, None] * N_KEYS + si[:, :, 1, None, :]).reshape(-1, PEER_HEADS, PEER_TOPK * PEER_TOPK)
    fv, fpos = lax.top_k(comb, PEER_TOPK)
    eid = jnp.take_along_axis(cidx, fpos, axis=-1)
    gate = jax.nn.softmax(fv, axis=-1).astype(h.dtype)

    nb = hf.shape[0] // PEER_BLOCK
    hb = hf.reshape(nb, PEER_BLOCK, D)
    eb = eid.reshape(nb, PEER_BLOCK, PEER_HEADS, PEER_TOPK)
    gb = gate.reshape(nb, PEER_BLOCK, PEER_HEADS, PEER_TOPK)

    def block(args):
        hx, ex, gx = args
        a = jnp.einsum('phkd,pd->phk', u_table[ex], hx)
        act = jax.nn.gelu(a, approximate=False) * gx
        return jnp.einsum('phk,phkd->pd', act, v_table[ex])

    out = lax.map(block, (hb, eb, gb)).reshape(-1, D)[:n]
    return out.reshape(B, T, D)


def decoder_layer(x, c, pool_prefix, conv_prefix, start, w_ada, b_ada, g_norm1, g_norm2,
                  w_in, b_in, w_pool, pool_scale, conv_w, conv_b, ln_g, ln_b, w_out, b_out,
                  w_query, sub_keys, u_table, v_table):
    mod = jax.nn.silu(c) @ w_ada + b_ada
    sh1, sc1, gt1, sh2, sc2, gt2 = [m[:, None, :] for m in jnp.split(mod, N_MOD, axis=-1)]
    h1 = rmsnorm(x, g_norm1) * (1 + sc1) + sh1
    mix, new_pool, new_conv = token_mixer(h1, pool_prefix, conv_prefix, start, w_in, b_in,
                                          w_pool, pool_scale, conv_w, conv_b, ln_g, ln_b,
                                          w_out, b_out)
    x = x + gt1 * mix
    h2 = rmsnorm(x, g_norm2) * (1 + sc2) + sh2
    x = x + gt2 * peer_ffn(h2, w_query, sub_keys, u_table, v_table)
    return x, new_pool, new_conv


def setup_inputs(seed: int = 0) -> dict:
    key = jax.random.key(seed)
    ks = jax.random.split(key, 24)
    f32 = jnp.float32
    nrm = lambda k, shp, s: (jax.random.normal(k, shp, f32) * s).astype(f32)
    D = D_MODEL
    return {
        'x_prompt': nrm(ks[0], (BATCH, SEQ, D), 1.0),
        'x_sample': nrm(ks[1], (DEC_BATCH, DEC_SEQ, D), 1.0),
        'state_pool': nrm(ks[2], (DEPTH, DEC_BATCH, POOL_STATE, POOL_WIDTH), 1.0),
        'state_conv': nrm(ks[3], (DEPTH, DEC_BATCH, CONV_STATE, CONV_WIDTH), 0.5),
        'c_prompt': nrm(ks[4], (BATCH, D), 1.0),
        'c_sample': nrm(ks[5], (DEC_BATCH, D), 1.0),
        'w_ada': nrm(ks[6], (DEPTH, D, N_MOD * D), 0.5 * D ** -0.5),
        'b_ada': nrm(ks[7], (DEPTH, N_MOD * D), 0.02),
        'g_norm1': 1.0 + nrm(ks[8], (DEPTH, D), 0.05),
        'g_norm2': 1.0 + nrm(ks[9], (DEPTH, D), 0.05),
        'w_in': nrm(ks[10], (DEPTH, D, N_IN), D ** -0.5),
        'b_in': nrm(ks[11], (DEPTH, N_IN), 0.02),
        'w_pool': nrm(ks[12], (DEPTH, N_POOL_GROUPS, POOL_GROUP, POOL_GROUP), POOL_GROUP ** -0.5),
        'pool_scale': 1.0 + nrm(ks[13], (DEPTH, POOL_WIDTH), 0.1),
        'conv_w': nrm(ks[14], (DEPTH, CONV_KERNEL, 1, CONV_WIDTH), CONV_KERNEL ** -0.5),
        'conv_b': nrm(ks[15], (DEPTH, CONV_WIDTH), 0.02),
        'ln_g': 1.0 + nrm(ks[16], (DEPTH, CONV_WIDTH), 0.05),
        'ln_b': nrm(ks[17], (DEPTH, CONV_WIDTH), 0.02),
        'w_out': nrm(ks[18], (DEPTH, MIX_WIDTH, D), MIX_WIDTH ** -0.5),
        'b_out': nrm(ks[19], (DEPTH, D), 0.02),
        'w_query': nrm(ks[20], (DEPTH, D, PEER_HEADS * PEER_KEY_DIM), D ** -0.5),
        'sub_keys': nrm(ks[21], (DEPTH, PEER_HEADS, 2, N_KEYS, PEER_HALF), PEER_HALF ** -0.5),
        'u_table': nrm(ks[22], (DEPTH, N_EXPERTS, D), D ** -0.5),
        'v_table': nrm(ks[23], (DEPTH, N_EXPERTS, D), PEER_HEADS ** -0.5),
        'g_final': 1.0 + nrm(jax.random.fold_in(key, 99), (D,), 0.05),
    }


def reference(x_prompt, x_sample, state_pool, state_conv, c_prompt, c_sample, w_ada, b_ada,
              g_norm1, g_norm2, w_in, b_in, w_pool, pool_scale, conv_w, conv_b, ln_g, ln_b,
              w_out, b_out, w_query, sub_keys, u_table, v_table, g_final):
    xp, xs = x_prompt, x_sample
    pool_p, conv_p, pool_s, conv_s = [], [], [], []
    for l in range(DEPTH):
        zero_pool = jnp.zeros((xp.shape[0], POOL_STATE, POOL_WIDTH), xp.dtype)
        zero_conv = jnp.zeros((xp.shape[0], CONV_STATE, CONV_WIDTH), xp.dtype)
        xp, npp, ncp = decoder_layer(xp, c_prompt, zero_pool, zero_conv, 0,
                                     w_ada[l], b_ada[l], g_norm1[l], g_norm2[l], w_in[l], b_in[l],
                                     w_pool[l], pool_scale[l], conv_w[l], conv_b[l], ln_g[l], ln_b[l],
                                     w_out[l], b_out[l], w_query[l], sub_keys[l], u_table[l], v_table[l])
        xs, nps, ncs = decoder_layer(xs, c_sample, state_pool[l], state_conv[l], PAST_LEN,
                                     w_ada[l], b_ada[l], g_norm1[l], g_norm2[l], w_in[l], b_in[l],
                                     w_pool[l], pool_scale[l], conv_w[l], conv_b[l], ln_g[l], ln_b[l],
                                     w_out[l], b_out[l], w_query[l], sub_keys[l], u_table[l], v_table[l])
        pool_p.append(npp); conv_p.append(ncp); pool_s.append(nps); conv_s.append(ncs)
    y_prompt = rmsnorm(xp, g_final)
    y_sample = rmsnorm(xs, g_final)
    return (y_prompt, y_sample, jnp.stack(pool_p), jnp.stack(conv_p), jnp.stack(pool_s), jnp.stack(conv_s))
```

```python
import functools

import jax
import jax.numpy as jnp
from jax import lax
from jax.experimental import pallas as pl
from jax.experimental.pallas import tpu as pltpu

EPS = 1e-6
POOL_WINDOWS = (2, 4, 8, 16)
PEER_TOPK = 16
PAST_LEN = 16384
SQRT_HALF = 0.7071067811865476

LANES = 128
SUBLANES = 8
VMEM_LIMIT_BYTES = 48 * 1024 * 1024
ROW_BLOCK = 512
EXPERT_TILE = 1024

F32 = jnp.float32
BF16 = jnp.bfloat16


def _params(*sem):
    return pltpu.CompilerParams(dimension_semantics=sem, vmem_limit_bytes=VMEM_LIMIT_BYTES)


def _full(shape):
    return pl.BlockSpec(shape, lambda *_: (0,) * len(shape))


def _mod_kernel(c_ref, w_ref, b_ref, o_ref):
    c = c_ref[...]
    act = (c * jax.nn.sigmoid(c)).astype(BF16)
    o_ref[...] = jnp.dot(act, w_ref[...].astype(BF16), preferred_element_type=F32) + b_ref[...]


def _adaln_mod(c, w, b):
    rows, d = c.shape
    n = w.shape[1]
    tn = min(n, 1024)
    return pl.pallas_call(
        _mod_kernel,
        grid=(n // tn,),
        in_specs=[_full((rows, d)), pl.BlockSpec((d, tn), lambda i: (0, i)), pl.BlockSpec((1, tn), lambda i: (0, i))],
        out_specs=pl.BlockSpec((rows, tn), lambda i: (0, i)),
        out_shape=jax.ShapeDtypeStruct((rows, n), F32),
        compiler_params=_params("arbitrary"),
        name="adaln_mod",
    )(c, w, b.reshape(1, n))


POOL_HIST = 16
CONV_HIST = 32
CONV_ROWS = 128
MIXER_BATCH_GROUP = 32


def _mixer_kernel(x_ref, mod_ref, pp_ref, cp_ref, g1_ref, g2_ref, win_ref, bin_ref, wpool_ref, pscale_ref,
                  cw_ref, cb_ref, lng_ref, lnb_ref, wout_ref, bout_ref,
                  x1_ref, h2_ref, npool_ref, nconv_ref,
                  zp_s, zc_s, conv_s, *, start, tt, nb, d, pw, cwid, kconv):
    step = pl.program_id(1)
    rows = tt * nb
    npast = max(POOL_WINDOWS) - 1
    cpast = kconv - 1
    pg = pw // len(POOL_WINDOWS)

    @pl.when(step == 0)
    def _():
        zp_s[(POOL_HIST - npast) * nb:POOL_HIST * nb] = pp_ref[...].reshape(npast * nb, pw)
        zc_s[(CONV_HIST - cpast) * nb:CONV_HIST * nb] = cp_ref[...].reshape(cpast * nb, cwid)

    sh1, sc1, gt1, sh2, sc2 = [mod_ref[:, k * d:(k + 1) * d][None] for k in range(5)]

    x3 = x_ref[...]
    ms = jnp.mean(x3 * x3, axis=-1, keepdims=True)
    h1 = (x3 * lax.rsqrt(ms + EPS) * g1_ref[...]) * (1.0 + sc1) + sh1
    proj = jnp.dot(h1.reshape(rows, d).astype(BF16), win_ref[...], preferred_element_type=F32) + bin_ref[...]
    hp = proj[:, :pw]
    cv = proj[:, pw:pw + cwid]
    cg = proj[:, pw + cwid:]

    zp_s[POOL_HIST * nb:(POOL_HIST + tt) * nb] = hp
    tpos = start + step * tt + lax.shift_right_logical(
        lax.broadcasted_iota(jnp.int32, (rows, pg), 0), nb.bit_length() - 1)
    pool_parts = []
    for gi, w in enumerate(POOL_WINDOWS):
        sl = slice(gi * pg, (gi + 1) * pg)
        cur = zp_s[POOL_HIST * nb:(POOL_HIST + tt) * nb, sl]
        wsum = cur
        for k in range(1, w):
            wsum = wsum + zp_s[(POOL_HIST - k) * nb:(POOL_HIST - k + tt) * nb, sl]
        cnt = jnp.minimum(tpos + 1, w).astype(F32)
        dg = wsum / cnt - cur
        po = jnp.dot(dg.astype(BF16), wpool_ref[gi], preferred_element_type=F32) * pscale_ref[:, sl]
        pool_parts.append(po)
    pool_out = jnp.concatenate(pool_parts, axis=-1)

    zc_s[CONV_HIST * nb:(CONV_HIST + tt) * nb] = cv * jax.nn.sigmoid(cg)

    def conv_chunk(ci, carry):
        r0 = pl.multiple_of(ci * CONV_ROWS, CONV_ROWS)
        for ct in range(cwid // LANES):
            cs = slice(ct * LANES, (ct + 1) * LANES)
            acc = jnp.zeros((CONV_ROWS, LANES), F32) + cb_ref[:, cs]
            for k in range(kconv):
                off = pl.multiple_of(r0 + (CONV_HIST - cpast + k) * nb, SUBLANES)
                acc = acc + zc_s[pl.ds(off, CONV_ROWS), cs] * cw_ref[k:k + 1, cs]
            conv_s[pl.ds(r0, CONV_ROWS), cs] = acc
        return carry

    lax.fori_loop(0, rows // CONV_ROWS, conv_chunk, 0)
    conv = conv_s[...]
    mu = jnp.mean(conv, axis=-1, keepdims=True)
    cen = conv - mu
    var = jnp.mean(cen * cen, axis=-1, keepdims=True)
    ln = cen * lax.rsqrt(var + EPS) * lng_ref[...] + lnb_ref[...]
    conv_out = ln * jax.nn.sigmoid(ln)

    mixed = (jnp.dot(pool_out.astype(BF16), wout_ref[:pw], preferred_element_type=F32)
             + jnp.dot(conv_out.astype(BF16), wout_ref[pw:], preferred_element_type=F32) + bout_ref[...])
    x1 = x3 + gt1 * mixed.reshape(tt, nb, d)
    x1_ref[...] = x1
    ms2 = jnp.mean(x1 * x1, axis=-1, keepdims=True)
    h2 = (x1 * lax.rsqrt(ms2 + EPS) * g2_ref[...]) * (1.0 + sc2) + sh2
    h2_ref[...] = h2.astype(BF16)

    last_pool = zp_s[(POOL_HIST + tt - npast) * nb:(POOL_HIST + tt) * nb]
    last_conv = zc_s[(CONV_HIST + tt - cpast) * nb:(CONV_HIST + tt) * nb]
    zp_s[(POOL_HIST - npast) * nb:POOL_HIST * nb] = last_pool
    zc_s[(CONV_HIST - cpast) * nb:CONV_HIST * nb] = last_conv

    @pl.when(step == pl.num_programs(1) - 1)
    def _():
        npool_ref[...] = last_pool.reshape(npast, nb, pw)
        nconv_ref[...] = last_conv.reshape(cpast, nb, cwid)


def _mixer(x3d, mod, pool_prefix, conv_prefix, start, g1, g2, w_in, b_in, w_pool, pool_scale,
           conv_w, conv_b, ln_g, ln_b, w_out, b_out):
    t_total, nb_total, d = x3d.shape
    pw = pool_scale.shape[-1]
    kconv, cwid = conv_w.shape
    n_in = w_in.shape[1]
    npast = max(POOL_WINDOWS) - 1
    cpast = kconv - 1
    nb = min(nb_total, MIXER_BATCH_GROUP)
    tt = min(t_total, ROW_BLOCK // nb)
    rows = tt * nb
    assert nb % SUBLANES == 0 and nb & (nb - 1) == 0 and nb_total % nb == 0, (nb, nb_total)
    assert npast <= POOL_HIST and cpast <= CONV_HIST
    assert t_total % tt == 0 and rows % CONV_ROWS == 0, (rows, tt, t_total)
    body = functools.partial(_mixer_kernel, start=start, tt=tt, nb=nb, d=d, pw=pw, cwid=cwid, kconv=kconv)
    row_spec = pl.BlockSpec((tt, nb, d), lambda b, i: (i, b, 0))
    pool_spec = pl.BlockSpec((npast, nb, pw), lambda b, i: (0, b, 0))
    conv_spec = pl.BlockSpec((cpast, nb, cwid), lambda b, i: (0, b, 0))
    return pl.pallas_call(
        body,
        grid=(nb_total // nb, t_total // tt),
        in_specs=[row_spec, pl.BlockSpec((nb, mod.shape[1]), lambda b, i: (b, 0)), pool_spec, conv_spec,
                  _full((1, d)), _full((1, d)), _full(w_in.shape), _full((1, n_in)), _full(w_pool.shape),
                  _full((1, pw)), _full(conv_w.shape), _full((1, cwid)), _full((1, cwid)), _full((1, cwid)),
                  _full(w_out.shape), _full((1, d))],
        out_specs=[row_spec, row_spec, pool_spec, conv_spec],
        out_shape=[jax.ShapeDtypeStruct(x3d.shape, F32), jax.ShapeDtypeStruct(x3d.shape, BF16),
                   jax.ShapeDtypeStruct(pool_prefix.shape, F32), jax.ShapeDtypeStruct(conv_prefix.shape, F32)],
        scratch_shapes=[pltpu.VMEM(((POOL_HIST + tt) * nb, pw), F32),
                        pltpu.VMEM(((CONV_HIST + tt) * nb, cwid), F32),
                        pltpu.VMEM((rows, cwid), F32)],
        compiler_params=_params("arbitrary", "arbitrary"),
        name="mixer",
    )(x3d, mod, pool_prefix, conv_prefix, g1.reshape(1, d), g2.reshape(1, d), w_in, b_in.reshape(1, n_in),
      w_pool, pool_scale.reshape(1, pw), conv_w, conv_b.reshape(1, cwid), ln_g.reshape(1, cwid),
      ln_b.reshape(1, cwid), w_out, b_out.reshape(1, d))


def _ce(a, b):
    if b is None:
        return a, None
    if a is None:
        return b, None
    return jnp.maximum(a, b), jnp.minimum(a, b)


def _oddeven_merge(lo, hi, r):
    step = r * 2
    if step < hi - lo:
        yield from _oddeven_merge(lo, hi, step)
        yield from _oddeven_merge(lo + r, hi, step)
        yield from [(i, i + r) for i in range(lo + r, hi - r, step)]
    else:
        yield (lo, lo + r)


def _oddeven_sort_pairs(lo, hi):
    if hi - lo >= 1:
        mid = lo + (hi - lo) // 2
        yield from _oddeven_sort_pairs(lo, mid)
        yield from _oddeven_sort_pairs(mid + 1, hi)
        yield from _oddeven_merge(lo, hi, 1)


def _sort_desc(vals):
    vals = list(vals)
    for i, j in _oddeven_sort_pairs(0, len(vals) - 1):
        vals[i], vals[j] = _ce(vals[i], vals[j])
    return vals


def _bitonic_to_desc(vals):
    vals = list(vals)
    n = len(vals)
    dist = n // 2
    while dist >= 1:
        for i in range(n):
            if i & dist == 0:
                vals[i], vals[i + dist] = _ce(vals[i], vals[i + dist])
        dist //= 2
    return vals


def _merge_top(a, b, k):
    a = list(a) + [None] * (k - len(a))
    b = list(b) + [None] * (k - len(b))
    return _bitonic_to_desc([_ce(a[r], b[k - 1 - r])[0] for r in range(k)])


def _top_sorted_over_keys(blocks, k):
    assert len(blocks) == k
    vals = _sort_desc(blocks)
    shift = SUBLANES // 2
    while shift >= 1:
        vals = _merge_top(vals, [pltpu.roll(v, shift, 0) for v in vals], k)
        shift //= 2
    return vals


def _candidate_chains(k):
    ok = lambda a, b: (a + 1) * (b + 1) <= k
    chains = []
    dgl = 0
    while ok(dgl, dgl):
        chains.append([(dgl, b) for b in range(dgl, k) if ok(dgl, b)])
        col = [(a, dgl) for a in range(dgl + 1, k) if ok(a, dgl)]
        if col:
            chains.append(col)
        dgl += 1
    return chains


def _router_kernel(h2_ref, wqt_ref, keys_ref, s1_ref, e1_ref, th_ref, ez_ref, q_s, s_s,
                   *, heads, nkeys, half, tb):
    k = PEER_TOPK
    q_s[...] = lax.dot_general(wqt_ref[...], h2_ref[...], (((1,), (1,)), ((), ())),
                               preferred_element_type=F32).astype(BF16)
    chains = _candidate_chains(k)
    nblk = nkeys // SUBLANES

    def head_body(h, carry):
        for p in range(2):
            qrow = pl.multiple_of((h * 2 + p) * half, half)
            s_s[p] = jnp.dot(keys_ref[h * 2 + p], q_s[pl.ds(qrow, half), :], preferred_element_type=F32)

        def group_body(g, carry2):
            lanes = pl.ds(pl.multiple_of(g * LANES, LANES), LANES)
            blk = [[s_s[p, kb * SUBLANES:(kb + 1) * SUBLANES, lanes] for kb in range(nblk)] for p in range(2)]
            v0 = _top_sorted_over_keys(blk[0], k)
            v1 = _top_sorted_over_keys(blk[1], k)
            comb = {}
            sorted_chains = []
            for chain in chains:
                for (a, b) in chain:
                    comb[(a, b)] = v0[a] + v1[b]
                sorted_chains.append([comb[c] for c in chain])
            sorted_chains.sort(key=len)
            while len(sorted_chains) > 1:
                m = _merge_top(sorted_chains[0], sorted_chains[1], k)
                m = [x for x in m if x is not None]
                sorted_chains = sorted(sorted_chains[2:] + [m], key=len)
            tau = sorted_chains[0][k - 1]
            e0 = [jnp.exp(v - v0[0]) for v in v0]
            e1 = [jnp.exp(v - v1[0]) for v in v1]
            inf = jnp.full((SUBLANES, LANES), jnp.inf, F32)
            theta = [inf] * k
            z = jnp.zeros((SUBLANES, LANES), F32)
            for (a, b), c in comb.items():
                sel = c >= tau
                theta[a] = jnp.minimum(theta[a], jnp.where(sel, v1[b], jnp.inf))
                z = z + jnp.where(sel, e0[a] * e1[b], 0.0)
            inv_z = 1.0 / z
            for kb in range(nblk):
                rs = slice(kb * SUBLANES, (kb + 1) * SUBLANES)
                s0b = blk[0][kb]
                th = inf
                for a in range(k):
                    th = jnp.where(s0b == v0[a], theta[a], th)
                th_ref[h, rs, lanes] = th
                ez_ref[h, rs, lanes] = jnp.exp(s0b - v0[0]) * inv_z
                s1b = blk[1][kb]
                s1_ref[h, rs, lanes] = s1b
                e1_ref[h, rs, lanes] = jnp.exp(s1b - v1[0])
            return carry2

        lax.fori_loop(0, tb // LANES, group_body, 0)
        return carry

    lax.fori_loop(0, heads, head_body, 0)


def _router(h2, wqt, keys, tb):
    n, d = h2.shape
    hk, nkeys, half = keys.shape
    heads = hk // 2
    assert nkeys == PEER_TOPK * SUBLANES and half % LANES == 0, (nkeys, half)
    body = functools.partial(_router_kernel, heads=heads, nkeys=nkeys, half=half, tb=tb)
    out_spec = pl.BlockSpec((heads, nkeys, tb), lambda i: (0, 0, i))
    out = jax.ShapeDtypeStruct((heads, nkeys, n), F32)
    return pl.pallas_call(
        body,
        grid=(n // tb,),
        in_specs=[pl.BlockSpec((tb, d), lambda i: (i, 0)), _full(wqt.shape), _full(keys.shape)],
        out_specs=[out_spec] * 4,
        out_shape=[out] * 4,
        scratch_shapes=[pltpu.VMEM((hk * half, tb), BF16), pltpu.VMEM((2, nkeys, tb), F32)],
        compiler_params=_params("arbitrary"),
        name="peer_router",
    )(h2, wqt, keys)


GATE_ROWS = 16


def _dense_kernel(h2_ref, u_ref, vt_ref, s1_ref, e1_ref, th_ref, ez_ref, x1_ref, gt2_ref, gf_ref, y_ref,
                  acc_s, a_s, act_s, *, heads, nkeys, tb, te, d, nb):
    j = pl.program_id(1)
    key_rows = te // nkeys

    @pl.when(j == 0)
    def _():
        acc_s[...] = jnp.zeros_like(acc_s)

    a_s[...] = lax.dot_general(u_ref[...], h2_ref[...], (((1,), (1,)), ((), ())), preferred_element_type=F32)
    row0 = pl.multiple_of(j * key_rows, SUBLANES)

    def lane_group(g, carry):
        lanes = pl.ds(pl.multiple_of(g * LANES, LANES), LANES)
        th8 = [th_ref[h, pl.ds(row0, key_rows), lanes] for h in range(heads)]
        ez8 = [ez_ref[h, pl.ds(row0, key_rows), lanes] for h in range(heads)]
        for il in range(key_rows):
            th = [jnp.broadcast_to(t[il:il + 1], (GATE_ROWS, LANES)) for t in th8]
            ez = [jnp.broadcast_to(t[il:il + 1], (GATE_ROWS, LANES)) for t in ez8]
            for jb in range(nkeys // GATE_ROWS):
                js = slice(jb * GATE_ROWS, (jb + 1) * GATE_ROWS)
                w = jnp.zeros((GATE_ROWS, LANES), F32)
                for h in range(heads):
                    w = w + jnp.where(s1_ref[h, js, lanes] >= th[h], e1_ref[h, js, lanes] * ez[h], 0.0)
                arow = slice(il * nkeys + jb * GATE_ROWS, il * nkeys + (jb + 1) * GATE_ROWS)
                a = a_s[arow, lanes]
                gelu = 0.5 * a * (1.0 + lax.erf(a * SQRT_HALF))
                act_s[arow, lanes] = (gelu * w).astype(BF16)
        return carry

    lax.fori_loop(0, tb // LANES, lane_group, 0)
    acc_s[...] += jnp.dot(vt_ref[...], act_s[...], preferred_element_type=F32)

    @pl.when(j == pl.num_programs(1) - 1)
    def _():
        tt = tb // nb
        peer = acc_s[...].T.reshape(tt, nb, d)
        x2 = x1_ref[...].reshape(tt, nb, d) + gt2_ref[...][None] * peer
        ms = jnp.mean(x2 * x2, axis=-1, keepdims=True)
        y_ref[...] = (x2 * lax.rsqrt(ms + EPS) * gf_ref[...]).reshape(tb, d)


def _dense(h2, u, vt, s1, e1, th, ez, x1, gt2, g_final, tb, nb):
    n, d = h2.shape
    ne = u.shape[0]
    heads, nkeys, _ = s1.shape
    te = min(EXPERT_TILE, ne)
    assert ne == nkeys * nkeys and te == SUBLANES * nkeys and ne % te == 0 and nkeys % GATE_ROWS == 0
    body = functools.partial(_dense_kernel, heads=heads, nkeys=nkeys, tb=tb, te=te, d=d, nb=nb)
    tok = pl.BlockSpec((tb, d), lambda i, j: (i, 0))
    gate = pl.BlockSpec((heads, nkeys, tb), lambda i, j: (0, 0, i))
    return pl.pallas_call(
        body,
        grid=(n // tb, ne // te),
        in_specs=[tok, pl.BlockSpec((te, d), lambda i, j: (j, 0)), pl.BlockSpec((d, te), lambda i, j: (0, j)),
                  gate, gate, gate, gate, tok, _full(gt2.shape), _full((1, d))],
        out_specs=tok,
        out_shape=jax.ShapeDtypeStruct((n, d), F32),
        scratch_shapes=[pltpu.VMEM((d, tb), F32), pltpu.VMEM((te, tb), F32), pltpu.VMEM((te, tb), BF16)],
        compiler_params=_params("arbitrary", "arbitrary"),
        name="peer_dense",
    )(h2, u, vt, s1, e1, th, ez, x1, gt2, g_final.reshape(1, d))


def _time_major(x):
    return jnp.transpose(x, (1, 0, 2))


def kernel(x_prompt, x_sample, state_pool, state_conv, c_prompt, c_sample, w_ada, b_ada, g_norm1, g_norm2,
           w_in, b_in, w_pool, pool_scale, conv_w, conv_b, ln_g, ln_b, w_out, b_out, w_query, sub_keys,
           u_table, v_table, g_final):
    depth = w_ada.shape[0]
    assert depth == 1, depth
    bp, _, d = x_prompt.shape
    pw = pool_scale.shape[-1]
    cwid = conv_b.shape[-1]
    heads, _, nkeys, half = sub_keys.shape[1:]
    npast = max(POOL_WINDOWS) - 1
    cpast = conv_w.shape[1] - 1

    mod = _adaln_mod(jnp.concatenate([c_prompt, c_sample], axis=0), w_ada[0], b_ada[0])
    weights = (g_norm1[0], g_norm2[0], w_in[0].astype(BF16), b_in[0], w_pool[0].astype(BF16), pool_scale[0],
               conv_w[0].reshape(conv_w.shape[1], cwid), conv_b[0], ln_g[0], ln_b[0], w_out[0].astype(BF16),
               b_out[0])
    wqt = w_query[0].T.astype(BF16)
    keys = sub_keys[0].reshape(heads * 2, nkeys, half).astype(BF16)
    u = u_table[0].astype(BF16)
    vt = v_table[0].T.astype(BF16)

    streams = (
        (_time_major(x_prompt), mod[:bp], jnp.zeros((npast, bp, pw), F32), jnp.zeros((cpast, bp, cwid), F32), 0),
        (_time_major(x_sample), mod[bp:], _time_major(state_pool[0]), _time_major(state_conv[0]), PAST_LEN),
    )
    ys, pools, convs = [], [], []
    for x3d, m, pool_prefix, conv_prefix, start in streams:
        t_total, nb, _ = x3d.shape
        n = t_total * nb
        x1, h2, new_pool, new_conv = _mixer(x3d, m, pool_prefix, conv_prefix, start, *weights)
        tb = min(ROW_BLOCK, n)
        h2 = h2.reshape(n, d)
        s1, e1, th, ez = _router(h2, wqt, keys, tb)
        y = _dense(h2, u, vt, s1, e1, th, ez, x1.reshape(n, d), m[:, 5 * d:6 * d], g_final, tb, nb)
        ys.append(_time_major(y.reshape(t_total, nb, d)))
        pools.append(_time_major(new_pool)[None])
        convs.append(_time_major(new_conv)[None])
    return ys[0], ys[1], pools[0], convs[0], pools[1], convs[1]
```

```python
import functools

import jax
import jax.numpy as jnp
from jax import lax
from jax.experimental import pallas as pl
from jax.experimental.pallas import tpu as pltpu

EPS = 1e-6
POOL_WINDOWS = (2, 4, 8, 16)
PEER_TOPK = 16
PAST_LEN = 16384
SQRT_HALF = 0.7071067811865476

LANES = 128
SUBLANES = 8
VMEM_LIMIT_BYTES = 48 * 1024 * 1024
ROW_BLOCK = 512
EXPERT_TILE = 1024

F32 = jnp.float32
BF16 = jnp.bfloat16


def _params(*sem):
    return pltpu.CompilerParams(dimension_semantics=sem, vmem_limit_bytes=VMEM_LIMIT_BYTES)


def _full(shape):
    return pl.BlockSpec(shape, lambda *_: (0,) * len(shape))


def _mod_kernel(c_ref, w_ref, b_ref, o_ref):
    c = c_ref[...]
    act = (c * jax.nn.sigmoid(c)).astype(BF16)
    o_ref[...] = jnp.dot(act, w_ref[...].astype(BF16), preferred_element_type=F32) + b_ref[...]


def _adaln_mod(c, w, b):
    rows, d = c.shape
    n = w.shape[1]
    tn = min(n, 1024)
    return pl.pallas_call(
        _mod_kernel,
        grid=(n // tn,),
        in_specs=[_full((rows, d)), pl.BlockSpec((d, tn), lambda i: (0, i)), pl.BlockSpec((1, tn), lambda i: (0, i))],
        out_specs=pl.BlockSpec((rows, tn), lambda i: (0, i)),
        out_shape=jax.ShapeDtypeStruct((rows, n), F32),
        compiler_params=_params("arbitrary"),
        name="adaln_mod",
    )(c, w, b.reshape(1, n))


POOL_HIST = 16
CONV_HIST = 32
CONV_ROWS = 128
MIXER_BATCH_GROUP = 32


def _mixer_kernel(x_ref, mod_ref, pp_ref, cp_ref, g1_ref, g2_ref, win_ref, bin_ref, wpool_ref, pscale_ref,
                  cw_ref, cb_ref, lng_ref, lnb_ref, wout_ref, bout_ref,
                  x1_ref, h2_ref, npool_ref, nconv_ref,
                  zp_s, zc_s, conv_s, *, start, tt, nb, d, pw, cwid, kconv):
    step = pl.program_id(1)
    rows = tt * nb
    npast = max(POOL_WINDOWS) - 1
    cpast = kconv - 1
    pg = pw // len(POOL_WINDOWS)

    @pl.when(step == 0)
    def _():
        zp_s[(POOL_HIST - npast) * nb:POOL_HIST * nb] = pp_ref[...].reshape(npast * nb, pw)
        zc_s[(CONV_HIST - cpast) * nb:CONV_HIST * nb] = cp_ref[...].reshape(cpast * nb, cwid)

    sh1, sc1, gt1, sh2, sc2 = [mod_ref[:, k * d:(k + 1) * d][None] for k in range(5)]

    x3 = x_ref[...]
    ms = jnp.mean(x3 * x3, axis=-1, keepdims=True)
    h1 = (x3 * lax.rsqrt(ms + EPS) * g1_ref[...]) * (1.0 + sc1) + sh1
    proj = jnp.dot(h1.reshape(rows, d).astype(BF16), win_ref[...], preferred_element_type=F32) + bin_ref[...]
    hp = proj[:, :pw]
    cv = proj[:, pw:pw + cwid]
    cg = proj[:, pw + cwid:]

    zp_s[POOL_HIST * nb:(POOL_HIST + tt) * nb] = hp
    tpos = start + step * tt + lax.shift_right_logical(
        lax.broadcasted_iota(jnp.int32, (rows, pg), 0), nb.bit_length() - 1)
    pool_parts = []
    for gi, w in enumerate(POOL_WINDOWS):
        sl = slice(gi * pg, (gi + 1) * pg)
        cur = zp_s[POOL_HIST * nb:(POOL_HIST + tt) * nb, sl]
        wsum = cur
        for k in range(1, w):
            wsum = wsum + zp_s[(POOL_HIST - k) * nb:(POOL_HIST - k + tt) * nb, sl]
        cnt = jnp.minimum(tpos + 1, w).astype(F32)
        dg = wsum / cnt - cur
        po = jnp.dot(dg.astype(BF16), wpool_ref[gi], preferred_element_type=F32) * pscale_ref[:, sl]
        pool_parts.append(po)
    pool_out = jnp.concatenate(pool_parts, axis=-1)

    zc_s[CONV_HIST * nb:(CONV_HIST + tt) * nb] = cv * jax.nn.sigmoid(cg)

    def conv_chunk(ci, carry):
        r0 = pl.multiple_of(ci * CONV_ROWS, CONV_ROWS)
        for ct in range(cwid // LANES):
            cs = slice(ct * LANES, (ct + 1) * LANES)
            acc = jnp.zeros((CONV_ROWS, LANES), F32) + cb_ref[:, cs]
            for k in range(kconv):
                off = pl.multiple_of(r0 + (CONV_HIST - cpast + k) * nb, SUBLANES)
                acc = acc + zc_s[pl.ds(off, CONV_ROWS), cs] * cw_ref[k:k + 1, cs]
            conv_s[pl.ds(r0, CONV_ROWS), cs] = acc
        return carry

    lax.fori_loop(0, rows // CONV_ROWS, conv_chunk, 0)
    conv = conv_s[...]
    mu = jnp.mean(conv, axis=-1, keepdims=True)
    cen = conv - mu
    var = jnp.mean(cen * cen, axis=-1, keepdims=True)
    ln = cen * lax.rsqrt(var + EPS) * lng_ref[...] + lnb_ref[...]
    conv_out = ln * jax.nn.sigmoid(ln)

    mixed = (jnp.dot(pool_out.astype(BF16), wout_ref[:pw], preferred_element_type=F32)
             + jnp.dot(conv_out.astype(BF16), wout_ref[pw:], preferred_element_type=F32) + bout_ref[...])
    x1 = x3 + gt1 * mixed.reshape(tt, nb, d)
    x1_ref[...] = x1
    ms2 = jnp.mean(x1 * x1, axis=-1, keepdims=True)
    h2 = (x1 * lax.rsqrt(ms2 + EPS) * g2_ref[...]) * (1.0 + sc2) + sh2
    h2_ref[...] = h2.astype(BF16)

    last_pool = zp_s[(POOL_HIST + tt - npast) * nb:(POOL_HIST + tt) * nb]
    last_conv = zc_s[(CONV_HIST + tt - cpast) * nb:(CONV_HIST + tt) * nb]
    zp_s[(POOL_HIST - npast) * nb:POOL_HIST * nb] = last_pool
    zc_s[(CONV_HIST - cpast) * nb:CONV_HIST * nb] = last_conv

    @pl.when(step == pl.num_programs(1) - 1)
    def _():
        npool_ref[...] = last_pool.reshape(npast, nb, pw)
        nconv_ref[...] = last_conv.reshape(cpast, nb, cwid)


def _mixer(x3d, mod, pool_prefix, conv_prefix, start, g1, g2, w_in, b_in, w_pool, pool_scale,
           conv_w, conv_b, ln_g, ln_b, w_out, b_out):
    t_total, nb_total, d = x3d.shape
    pw = pool_scale.shape[-1]
    kconv, cwid = conv_w.shape
    n_in = w_in.shape[1]
    npast = max(POOL_WINDOWS) - 1
    cpast = kconv - 1
    nb = min(nb_total, MIXER_BATCH_GROUP)
    tt = min(t_total, ROW_BLOCK // nb)
    rows = tt * nb
    assert nb % SUBLANES == 0 and nb & (nb - 1) == 0 and nb_total % nb == 0, (nb, nb_total)
    assert npast <= POOL_HIST and cpast <= CONV_HIST
    assert t_total % tt == 0 and rows % CONV_ROWS == 0, (rows, tt, t_total)
    body = functools.partial(_mixer_kernel, start=start, tt=tt, nb=nb, d=d, pw=pw, cwid=cwid, kconv=kconv)
    row_spec = pl.BlockSpec((tt, nb, d), lambda b, i: (i, b, 0))
    pool_spec = pl.BlockSpec((npast, nb, pw), lambda b, i: (0, b, 0))
    conv_spec = pl.BlockSpec((cpast, nb, cwid), lambda b, i: (0, b, 0))
    return pl.pallas_call(
        body,
        grid=(nb_total // nb, t_total // tt),
        in_specs=[row_spec, pl.BlockSpec((nb, mod.shape[1]), lambda b, i: (b, 0)), pool_spec, conv_spec,
                  _full((1, d)), _full((1, d)), _full(w_in.shape), _full((1, n_in)), _full(w_pool.shape),
                  _full((1, pw)), _full(conv_w.shape), _full((1, cwid)), _full((1, cwid)), _full((1, cwid)),
                  _full(w_out.shape), _full((1, d))],
        out_specs=[row_spec, row_spec, pool_spec, conv_spec],
        out_shape=[jax.ShapeDtypeStruct(x3d.shape, F32), jax.ShapeDtypeStruct(x3d.shape, BF16),
                   jax.ShapeDtypeStruct(pool_prefix.shape, F32), jax.ShapeDtypeStruct(conv_prefix.shape, F32)],
        scratch_shapes=[pltpu.VMEM(((POOL_HIST + tt) * nb, pw), F32),
                        pltpu.VMEM(((CONV_HIST + tt) * nb, cwid), F32),
                        pltpu.VMEM((rows, cwid), F32)],
        compiler_params=_params("arbitrary", "arbitrary"),
        name="mixer",
    )(x3d, mod, pool_prefix, conv_prefix, g1.reshape(1, d), g2.reshape(1, d), w_in, b_in.reshape(1, n_in),
      w_pool, pool_scale.reshape(1, pw), conv_w, conv_b.reshape(1, cwid), ln_g.reshape(1, cwid),
      ln_b.reshape(1, cwid), w_out, b_out.reshape(1, d))


def _ce(a, b):
    if b is None:
        return a, None
    if a is None:
        return b, None
    return jnp.maximum(a, b), jnp.minimum(a, b)


def _oddeven_merge(lo, hi, r):
    step = r * 2
    if step < hi - lo:
        yield from _oddeven_merge(lo, hi, step)
        yield from _oddeven_merge(lo + r, hi, step)
        yield from [(i, i + r) for i in range(lo + r, hi - r, step)]
    else:
        yield (lo, lo + r)


def _oddeven_sort_pairs(lo, hi):
    if hi - lo >= 1:
        mid = lo + (hi - lo) // 2
        yield from _oddeven_sort_pairs(lo, mid)
        yield from _oddeven_sort_pairs(mid + 1, hi)
        yield from _oddeven_merge(lo, hi, 1)


def _sort_desc(vals):
    vals = list(vals)
    for i, j in _oddeven_sort_pairs(0, len(vals) - 1):
        vals[i], vals[j] = _ce(vals[i], vals[j])
    return vals


def _bitonic_to_desc(vals):
    vals = list(vals)
    n = len(vals)
    dist = n // 2
    while dist >= 1:
        for i in range(n):
            if i & dist == 0:
                vals[i], vals[i + dist] = _ce(vals[i], vals[i + dist])
        dist //= 2
    return vals


def _merge_top(a, b, k):
    a = list(a) + [None] * (k - len(a))
    b = list(b) + [None] * (k - len(b))
    return _bitonic_to_desc([_ce(a[r], b[k - 1 - r])[0] for r in range(k)])


def _top_sorted_over_keys(blocks, k):
    assert len(blocks) == k
    vals = _sort_desc(blocks)
    shift = SUBLANES // 2
    while shift >= 1:
        vals = _merge_top(vals, [pltpu.roll(v, shift, 0) for v in vals], k)
        shift //= 2
    return vals


def _candidate_chains(k):
    ok = lambda a, b: (a + 1) * (b + 1) <= k
    chains = []
    dgl = 0
    while ok(dgl, dgl):
        chains.append([(dgl, b) for b in range(dgl, k) if ok(dgl, b)])
        col = [(a, dgl) for a in range(dgl + 1, k) if ok(a, dgl)]
        if col:
            chains.append(col)
        dgl += 1
    return chains


def _router_kernel(h2_ref, wqt_ref, keys_ref, s1_ref, e1_ref, th_ref, ez_ref, q_s, s_s,
                   *, heads, nkeys, half, tb):
    k = PEER_TOPK
    q_s[...] = lax.dot_general(wqt_ref[...], h2_ref[...], (((1,), (1,)), ((), ())),
                               preferred_element_type=F32).astype(BF16)
    chains = _candidate_chains(k)
    nblk = nkeys // SUBLANES

    def head_body(h, carry):
        for p in range(2):
            qrow = pl.multiple_of((h * 2 + p) * half, half)
            s_s[p] = jnp.dot(keys_ref[h * 2 + p], q_s[pl.ds(qrow, half), :], preferred_element_type=F32)

        def group_body(g, carry2):
            lanes = pl.ds(pl.multiple_of(g * LANES, LANES), LANES)
            blk = [[s_s[p, kb * SUBLANES:(kb + 1) * SUBLANES, lanes] for kb in range(nblk)] for p in range(2)]
            v0 = _top_sorted_over_keys(blk[0], k)
            v1 = _top_sorted_over_keys(blk[1], k)
            comb = {}
            sorted_chains = []
            for chain in chains:
                for (a, b) in chain:
                    comb[(a, b)] = v0[a] + v1[b]
                sorted_chains.append([comb[c] for c in chain])
            sorted_chains.sort(key=len)
            while len(sorted_chains) > 1:
                m = _merge_top(sorted_chains[0], sorted_chains[1], k)
                m = [x for x in m if x is not None]
                sorted_chains = sorted(sorted_chains[2:] + [m], key=len)
            tau = sorted_chains[0][k - 1]
            e0 = [jnp.exp(v - v0[0]) for v in v0]
            e1 = [jnp.exp(v - v1[0]) for v in v1]
            inf = jnp.full((SUBLANES, LANES), jnp.inf, F32)
            theta = [inf] * k
            z = jnp.zeros((SUBLANES, LANES), F32)
            for (a, b), c in comb.items():
                sel = c >= tau
                theta[a] = jnp.minimum(theta[a], jnp.where(sel, v1[b], jnp.inf))
                z = z + jnp.where(sel, e0[a] * e1[b], 0.0)
            inv_z = 1.0 / z
            for kb in range(nblk):
                rs = slice(kb * SUBLANES, (kb + 1) * SUBLANES)
                s0b = blk[0][kb]
                th = inf
                for a in range(k):
                    th = jnp.where(s0b == v0[a], theta[a], th)
                th_ref[h, g, rs, :] = th
                ez_ref[h, g, rs, :] = jnp.exp(s0b - v0[0]) * inv_z
                s1b = blk[1][kb]
                s1_ref[h, g, rs, :] = s1b
                e1_ref[h, g, rs, :] = jnp.exp(s1b - v1[0])
            return carry2

        lax.fori_loop(0, tb // LANES, group_body, 0)
        return carry

    lax.fori_loop(0, heads, head_body, 0)


def _router(h2, wqt, keys, tb):
    n, d = h2.shape
    hk, nkeys, half = keys.shape
    heads = hk // 2
    assert nkeys == PEER_TOPK * SUBLANES and half % LANES == 0, (nkeys, half)
    body = functools.partial(_router_kernel, heads=heads, nkeys=nkeys, half=half, tb=tb)
    out_spec = pl.BlockSpec((heads, tb // LANES, nkeys, LANES), lambda i: (0, i, 0, 0))
    out = jax.ShapeDtypeStruct((heads, n // LANES, nkeys, LANES), F32)
    return pl.pallas_call(
        body,
        grid=(n // tb,),
        in_specs=[pl.BlockSpec((tb, d), lambda i: (i, 0)), _full(wqt.shape), _full(keys.shape)],
        out_specs=[out_spec] * 4,
        out_shape=[out] * 4,
        scratch_shapes=[pltpu.VMEM((hk * half, tb), BF16), pltpu.VMEM((2, nkeys, tb), F32)],
        compiler_params=_params("arbitrary"),
        name="peer_router",
    )(h2, wqt, keys)


GATE_ROWS = 16


def _dense_kernel(h2_ref, u_ref, vt_ref, s1_ref, e1_ref, th_ref, ez_ref, x1_ref, gt2_ref, gf_ref, y_ref,
                  acc_s, a_s, act_s, *, heads, nkeys, tb, te, d, nb):
    j = pl.program_id(1)
    key_rows = te // nkeys

    @pl.when(j == 0)
    def _():
        acc_s[...] = jnp.zeros_like(acc_s)

    row0 = pl.multiple_of(j * key_rows, SUBLANES)
    groups = tb // LANES
    nsplit = 2 if groups % 2 == 0 else 1
    gps = groups // nsplit
    cols = gps * LANES

    for sp in range(nsplit):
        a = lax.dot_general(u_ref[...], h2_ref[sp * cols:(sp + 1) * cols, :], (((1,), (1,)), ((), ())),
                            preferred_element_type=F32)
        for gg in range(gps):
            a_s[sp * gps + gg] = a[:, gg * LANES:(gg + 1) * LANES]

    def gate_group(g):
        th8 = [th_ref[h, g, pl.ds(row0, key_rows), :] for h in range(heads)]
        ez8 = [ez_ref[h, g, pl.ds(row0, key_rows), :] for h in range(heads)]
        for il in range(key_rows):
            th = [jnp.broadcast_to(t[il:il + 1], (GATE_ROWS, LANES)) for t in th8]
            ez = [jnp.broadcast_to(t[il:il + 1], (GATE_ROWS, LANES)) for t in ez8]
            for jb in range(nkeys // GATE_ROWS):
                js = slice(jb * GATE_ROWS, (jb + 1) * GATE_ROWS)
                w = jnp.zeros((GATE_ROWS, LANES), F32)
                for h in range(heads):
                    w = w + jnp.where(s1_ref[h, g, js, :] >= th[h], e1_ref[h, g, js, :] * ez[h], 0.0)
                arow = slice(il * nkeys + jb * GATE_ROWS, il * nkeys + (jb + 1) * GATE_ROWS)
                a = a_s[g, arow, :]
                gelu = 0.5 * a * (1.0 + lax.erf(a * SQRT_HALF))
                act_s[g, arow, :] = (gelu * w).astype(BF16)

    for sp in range(nsplit):
        for gg in range(gps):
            gate_group(sp * gps + gg)
        act = jnp.concatenate([act_s[sp * gps + gg] for gg in range(gps)], axis=1)
        acc_s[:, sp * cols:(sp + 1) * cols] += jnp.dot(vt_ref[...], act, preferred_element_type=F32)

    @pl.when(j == pl.num_programs(1) - 1)
    def _():
        tt = tb // nb
        peer = acc_s[...].T.reshape(tt, nb, d)
        x2 = x1_ref[...].reshape(tt, nb, d) + gt2_ref[...][None] * peer
        ms = jnp.mean(x2 * x2, axis=-1, keepdims=True)
        y_ref[...] = (x2 * lax.rsqrt(ms + EPS) * gf_ref[...]).reshape(tb, d)


def _dense(h2, u, vt, s1, e1, th, ez, x1, gt2, g_final, tb, nb):
    n, d = h2.shape
    ne = u.shape[0]
    heads, _, nkeys, _ = s1.shape
    te = min(EXPERT_TILE, ne)
    assert ne == nkeys * nkeys and te == SUBLANES * nkeys and ne % te == 0 and nkeys % GATE_ROWS == 0
    body = functools.partial(_dense_kernel, heads=heads, nkeys=nkeys, tb=tb, te=te, d=d, nb=nb)
    tok = pl.BlockSpec((tb, d), lambda i, j: (i, 0))
    gate = pl.BlockSpec((heads, tb // LANES, nkeys, LANES), lambda i, j: (0, i, 0, 0))
    return pl.pallas_call(
        body,
        grid=(n // tb, ne // te),
        in_specs=[tok, pl.BlockSpec((te, d), lambda i, j: (j, 0)), pl.BlockSpec((d, te), lambda i, j: (0, j)),
                  gate, gate, gate, gate, tok, _full(gt2.shape), _full((1, d))],
        out_specs=tok,
        out_shape=jax.ShapeDtypeStruct((n, d), F32),
        scratch_shapes=[pltpu.VMEM((d, tb), F32), pltpu.VMEM((tb // LANES, te, LANES), F32),
                        pltpu.VMEM((tb // LANES, te, LANES), BF16)],
        compiler_params=_params("arbitrary", "arbitrary"),
        name="peer_dense",
    )(h2, u, vt, s1, e1, th, ez, x1, gt2, g_final.reshape(1, d))


def _time_major(x):
    return jnp.transpose(x, (1, 0, 2))


def kernel(x_prompt, x_sample, state_pool, state_conv, c_prompt, c_sample, w_ada, b_ada, g_norm1, g_norm2,
           w_in, b_in, w_pool, pool_scale, conv_w, conv_b, ln_g, ln_b, w_out, b_out, w_query, sub_keys,
           u_table, v_table, g_final):
    depth = w_ada.shape[0]
    assert depth == 1, depth
    bp, _, d = x_prompt.shape
    pw = pool_scale.shape[-1]
    cwid = conv_b.shape[-1]
    heads, _, nkeys, half = sub_keys.shape[1:]
    npast = max(POOL_WINDOWS) - 1
    cpast = conv_w.shape[1] - 1

    mod = _adaln_mod(jnp.concatenate([c_prompt, c_sample], axis=0), w_ada[0], b_ada[0])
    weights = (g_norm1[0], g_norm2[0], w_in[0].astype(BF16), b_in[0], w_pool[0].astype(BF16), pool_scale[0],
               conv_w[0].reshape(conv_w.shape[1], cwid), conv_b[0], ln_g[0], ln_b[0], w_out[0].astype(BF16),
               b_out[0])
    wqt = w_query[0].T.astype(BF16)
    keys = sub_keys[0].reshape(heads * 2, nkeys, half).astype(BF16)
    u = u_table[0].astype(BF16)
    vt = v_table[0].T.astype(BF16)

    streams = (
        (_time_major(x_prompt), mod[:bp], jnp.zeros((npast, bp, pw), F32), jnp.zeros((cpast, bp, cwid), F32), 0),
        (_time_major(x_sample), mod[bp:], _time_major(state_pool[0]), _time_major(state_conv[0]), PAST_LEN),
    )
    ys, pools, convs = [], [], []
    for x3d, m, pool_prefix, conv_prefix, start in streams:
        t_total, nb, _ = x3d.shape
        n = t_total * nb
        x1, h2, new_pool, new_conv = _mixer(x3d, m, pool_prefix, conv_prefix, start, *weights)
        tb = min(ROW_BLOCK, n)
        h2 = h2.reshape(n, d)
        s1, e1, th, ez = _router(h2, wqt, keys, tb)
        y = _dense(h2, u, vt, s1, e1, th, ez, x1.reshape(n, d), m[:, 5 * d:6 * d], g_final, tb, nb)
        ys.append(_time_major(y.reshape(t_total, nb, d)))
        pools.append(_time_major(new_pool)[None])
        convs.append(_time_major(new_conv)[None])
    return ys[0], ys[1], pools[0], convs[0], pools[1], convs[1]
```
